```python
import jax, jax.numpy as jnp
from jax import lax
import numpy as np

D_MODEL = 1024
BATCH = 8
SEQ = 4096
DEPTH = 2

PLE_DIM = 256
MIX_WIDTH = D_MODEL
GDN_WIDTH = MIX_WIDTH // 2
HGRN_WIDTH = MIX_WIDTH - GDN_WIDTH
HEAD_DIM = 128
GDN_HEADS = GDN_WIDTH // HEAD_DIM
HGRN_HEADS = HGRN_WIDTH // HEAD_DIM
CONV_WIDTH = 4
GDN_CHUNK = 64
HGRN_CHUNK = 16
D_FF = 4 * D_MODEL
EPS = 1e-6

OFF_GDN_QKV = 0
OFF_GDN_Z = OFF_GDN_QKV + 3 * GDN_WIDTH
OFF_GDN_A = OFF_GDN_Z + GDN_WIDTH
OFF_GDN_B = OFF_GDN_A + GDN_HEADS
OFF_HG_F = OFF_GDN_B + GDN_HEADS
OFF_HG_I = OFF_HG_F + HGRN_WIDTH
OFF_HG_Q = OFF_HG_I + HGRN_WIDTH
OFF_HG_G = OFF_HG_Q + HGRN_WIDTH
IN_COLS = OFF_HG_G + HGRN_WIDTH

kernel_name = 'hymba_gdn_hgrn2_sandwich_ple'


def rms_norm(x, w):
    xf = x.astype(jnp.float32)
    xf = xf * lax.rsqrt(jnp.mean(xf * xf, axis=-1, keepdims=True) + EPS)
    return (xf * w.astype(jnp.float32)).astype(x.dtype)


def gated_head_norm(o, w, z):
    o = o * lax.rsqrt(jnp.mean(o * o, axis=-1, keepdims=True) + EPS)
    return o * w.astype(jnp.float32) * jax.nn.silu(z)


def l2norm(t):
    return t * lax.rsqrt(jnp.sum(t * t, axis=-1, keepdims=True) + EPS)


def causal_depthwise_conv(x, w):
    k_w = w.shape[0]
    xp = jnp.pad(x, ((0, 0), (k_w - 1, 0), (0, 0)))
    return lax.conv_general_dilated(xp, w[:, None, :], window_strides=(1,), padding='VALID',
                                    dimension_numbers=('NWC', 'WIO', 'NWC'),
                                    feature_group_count=x.shape[-1])


def gated_delta_rule_chunked(q, k, v, g, beta):
    bsz, t_len, nh, dk = q.shape
    dv = v.shape[-1]
    c = GDN_CHUNK
    n = t_len // c
    q = q * (dk ** -0.5)
    q, k, v = (t.transpose(0, 2, 1, 3).reshape(bsz, nh, n, c, t.shape[-1]) for t in (q, k, v))
    g, beta = (t.transpose(0, 2, 1).reshape(bsz, nh, n, c) for t in (g, beta))
    gc = jnp.cumsum(g, axis=-1)
    causal = jnp.tril(jnp.ones((c, c), bool))
    strict = jnp.tril(jnp.ones((c, c), bool), -1)
    decay = jnp.exp(jnp.where(causal, gc[..., :, None] - gc[..., None, :], -jnp.inf))
    k_beta = k * beta[..., None]
    v_beta = v * beta[..., None]
    lmat = jnp.einsum('bhncd,bhnsd->bhncs', k_beta, k) * decay
    a_mat = jnp.eye(c, dtype=jnp.float32) + jnp.where(strict, lmat, 0.0)
    solve = lambda rhs: lax.linalg.triangular_solve(a_mat, rhs, left_side=True, lower=True,
                                                    unit_diagonal=True)
    u = solve(v_beta)
    w = solve(k_beta * jnp.exp(gc)[..., None])
    qk = jnp.einsum('bhncd,bhnsd->bhncs', q, k) * decay
    qg = q * jnp.exp(gc)[..., None]
    k_tail = k * jnp.exp(gc[..., -1:] - gc)[..., None]
    chunk_decay = jnp.exp(gc[..., -1])
    xs = tuple(jnp.moveaxis(t, 2, 0) for t in (u, w, qk, qg, k_tail, chunk_decay))

    def step(s, inp):
        u_n, w_n, qk_n, qg_n, kt_n, cd_n = inp
        v_new = u_n - jnp.einsum('bhcd,bhde->bhce', w_n, s)
        o = jnp.einsum('bhcd,bhde->bhce', qg_n, s) + jnp.einsum('bhcs,bhse->bhce', qk_n, v_new)
        s = s * cd_n[..., None, None] + jnp.einsum('bhcd,bhce->bhde', kt_n, v_new)
        return s, o

    s0 = jnp.zeros((bsz, nh, dk, dv), jnp.float32)
    _, o = lax.scan(step, s0, xs)
    return o.transpose(1, 0, 3, 2, 4).reshape(bsz, t_len, nh, dv)


def hgrn2_chunked(q, k, v, log_f):
    bsz, t_len, nh, dk = q.shape
    dv = v.shape[-1]
    c = HGRN_CHUNK
    n = t_len // c
    to_chunks = lambda t: jnp.moveaxis(t.reshape(bsz, n, c, nh, t.shape[-1]), 1, 0)
    bc = jnp.cumsum(to_chunks(log_f), axis=2)
    causal = jnp.tril(jnp.ones((c, c), bool))[None, :, :, None, None]

    def step(s, inp):
        q_n, k_n, v_n, b_n = inp
        pair = jnp.exp(jnp.where(causal, b_n[:, :, None] - b_n[:, None, :], -jnp.inf))
        att = jnp.einsum('bihd,bjhd,bijhd->bhij', q_n, k_n, pair)
        o = jnp.einsum('bhij,bjhe->bihe', att, v_n) + jnp.einsum('bihd,bhde->bihe', q_n * jnp.exp(b_n), s)
        b_last = b_n[:, -1]
        s = s * jnp.exp(b_last)[..., None] + jnp.einsum('bjhd,bjhe->bhde',
                                                         k_n * jnp.exp(b_last[:, None] - b_n), v_n)
        return s, o

    s0 = jnp.zeros((bsz, nh, dk, dv), jnp.float32)
    _, o = lax.scan(step, s0, (to_chunks(q), to_chunks(k), to_chunks(v), bc))
    return jnp.moveaxis(o, 0, 1).reshape(bsz, t_len, nh, dv)


def setup_inputs(seed: int = 0) -> dict:
    key = jax.random.key(seed)
    ks = jax.random.split(key, 20)
    f32 = jnp.float32
    nrm = lambda k, shape, scale: scale * jax.random.normal(k, shape, f32)
    gain = lambda k, shape: 1.0 + 0.02 * jax.random.normal(k, shape, f32)
    return {
        'x': jax.random.normal(ks[0], (BATCH, SEQ, D_MODEL), f32),
        'p': jax.random.normal(ks[1], (DEPTH, BATCH, SEQ, PLE_DIM), f32),
        'pre_mix_norm': gain(ks[2], (DEPTH, D_MODEL)),
        'w_in': nrm(ks[3], (DEPTH, D_MODEL, IN_COLS), D_MODEL ** -0.5),
        'gdn_conv': nrm(ks[4], (DEPTH, CONV_WIDTH, 3 * GDN_WIDTH), CONV_WIDTH ** -0.5),
        'gdn_a_log': jnp.log(jax.random.uniform(ks[5], (DEPTH, GDN_HEADS), f32, 1.0, 16.0)),
        'gdn_dt_bias': nrm(ks[6], (DEPTH, GDN_HEADS), 0.1),
        'gdn_norm': gain(ks[7], (DEPTH, HEAD_DIM)),
        'hgrn_lb_logits': nrm(ks[8], (DEPTH, HGRN_WIDTH), 1.0),
        'hgrn_norm': gain(ks[9], (DEPTH, HEAD_DIM)),
        'w_out': nrm(ks[10], (DEPTH, MIX_WIDTH, D_MODEL), MIX_WIDTH ** -0.5),
        'post_mix_norm': gain(ks[11], (DEPTH, D_MODEL)),
        'pre_mlp_norm': gain(ks[12], (DEPTH, D_MODEL)),
        'w_mlp_up': nrm(ks[13], (DEPTH, D_MODEL, D_FF), D_MODEL ** -0.5),
        'w_mlp_down': nrm(ks[14], (DEPTH, D_FF, D_MODEL), D_FF ** -0.5),
        'post_mlp_norm': gain(ks[15], (DEPTH, D_MODEL)),
        'w_ple_proj': nrm(ks[16], (DEPTH, PLE_DIM, D_MODEL), PLE_DIM ** -0.5),
        'ple_norm': gain(ks[17], (DEPTH, D_MODEL)),
        'w_ple_gate': nrm(ks[18], (DEPTH, D_MODEL, D_MODEL), D_MODEL ** -0.5),
    }


def reference(x, p, pre_mix_norm, w_in, gdn_conv, gdn_a_log, gdn_dt_bias, gdn_norm,
              hgrn_lb_logits, hgrn_norm, w_out, post_mix_norm, pre_mlp_norm, w_mlp_up,
              w_mlp_down, post_mlp_norm, w_ple_proj, ple_norm, w_ple_gate):
    f32 = jnp.float32
    bsz, t_len, _ = x.shape
    lb_cum = jnp.cumsum(jax.nn.softmax(hgrn_lb_logits.astype(f32), axis=0), axis=0)
    lower_bounds = lb_cum - lb_cum[0:1]
    h = x
    for i in range(DEPTH):
        hn = rms_norm(h, pre_mix_norm[i])
        proj = jnp.einsum('btd,dc->btc', hn, w_in[i]).astype(f32)
        qkv = proj[..., OFF_GDN_QKV:OFF_GDN_Z]
        z_gdn = proj[..., OFF_GDN_Z:OFF_GDN_A].reshape(bsz, t_len, GDN_HEADS, HEAD_DIM)
        a_gdn = proj[..., OFF_GDN_A:OFF_GDN_B]
        b_gdn = proj[..., OFF_GDN_B:OFF_HG_F]
        f_pre = proj[..., OFF_HG_F:OFF_HG_I].reshape(bsz, t_len, HGRN_HEADS, HEAD_DIM)
        i_hg = proj[..., OFF_HG_I:OFF_HG_Q].reshape(bsz, t_len, HGRN_HEADS, HEAD_DIM)
        q_hg = proj[..., OFF_HG_Q:OFF_HG_G].reshape(bsz, t_len, HGRN_HEADS, HEAD_DIM)
        g_hg = proj[..., OFF_HG_G:IN_COLS].reshape(bsz, t_len, HGRN_HEADS, HEAD_DIM)

        qkv = jax.nn.silu(causal_depthwise_conv(qkv, gdn_conv[i].astype(f32)))
        q_a, k_a, v_a = jnp.split(qkv, 3, axis=-1)
        q_a = l2norm(q_a.reshape(bsz, t_len, GDN_HEADS, HEAD_DIM))
        k_a = l2norm(k_a.reshape(bsz, t_len, GDN_HEADS, HEAD_DIM))
        v_a = v_a.reshape(bsz, t_len, GDN_HEADS, HEAD_DIM)
        beta = jax.nn.sigmoid(b_gdn)
        g_dec = -jnp.exp(gdn_a_log[i].astype(f32)) * jax.nn.softplus(a_gdn + gdn_dt_bias[i].astype(f32))
        o_a = gated_delta_rule_chunked(q_a, k_a, v_a, g_dec, beta)
        o_a = gated_head_norm(o_a, gdn_norm[i], z_gdn)

        lb = lower_bounds[i].reshape(HGRN_HEADS, HEAD_DIM)
        log_f = jnp.logaddexp(jax.nn.log_sigmoid(f_pre), jnp.log(lb) + jax.nn.log_sigmoid(-f_pre))
        k_b = (1.0 - lb) * jax.nn.sigmoid(-f_pre)
        o_b = hgrn2_chunked(jax.nn.silu(q_hg), k_b, i_hg, log_f)
        o_b = gated_head_norm(o_b, hgrn_norm[i], g_hg)

        mix = jnp.concatenate([o_a.reshape(bsz, t_len, GDN_WIDTH),
                               o_b.reshape(bsz, t_len, HGRN_WIDTH)], axis=-1).astype(h.dtype)
        mix_out = jnp.einsum('btc,cd->btd', mix, w_out[i])
        h = h + rms_norm(mix_out, post_mix_norm[i])

        u = jnp.einsum('btd,df->btf', rms_norm(h, pre_mlp_norm[i]), w_mlp_up[i])
        y = jnp.einsum('btf,fd->btd', jnp.square(jax.nn.relu(u)), w_mlp_down[i])
        h = h + rms_norm(y, post_mlp_norm[i])

        e = rms_norm(jnp.einsum('bte,ed->btd', p[i], w_ple_proj[i]), ple_norm[i])
        gate = jax.nn.sigmoid(jnp.einsum('btd,de->bte', h, w_ple_gate[i]))
        h = h + e * gate
    return h
```

```python
import functools

import numpy as np
import jax
import jax.numpy as jnp
from jax import lax
from jax.experimental import pallas as pl
from jax.experimental.pallas import tpu as pltpu

F32 = jnp.float32
BF16 = jnp.bfloat16
EPS = 1e-6

HEAD_DIM = 128
N_HEADS = 4
GROUP_WIDTH = N_HEADS * HEAD_DIM
CONV_WIDTH = 4
GDN_CHUNK = 64
SEQ_BLOCK = 256
DIAG_BLOCK = 8
CONV_HALO = 8

TOKEN_TILE = 512
PROJ_COL_TILE = 1024
FF_TILE = 1024
VMEM_LIMIT_BYTES = 56 * 1024 * 1024


def _dot(a, b):
    return jnp.dot(a, b, preferred_element_type=F32)


def _dot_nt(a, b):
    return lax.dot_general(a, b, (((1,), (1,)), ((), ())), preferred_element_type=F32)


def _dot_tn(a, b):
    return lax.dot_general(a, b, (((0,), (0,)), ((), ())), preferred_element_type=F32)


def _split3(x):
    hi = x.astype(BF16)
    r1 = x - hi.astype(F32)
    mid = r1.astype(BF16)
    r2 = r1 - mid.astype(F32)
    return hi, mid, r2.astype(BF16)


def _dot_sel_left(sel16, x):
    hi, mid, lo = _split3(x)
    return _dot(sel16, hi) + _dot(sel16, mid) + _dot(sel16, lo)


def _dot_sel_right(x, sel16):
    hi, mid, lo = _split3(x)
    return _dot(hi, sel16) + _dot(mid, sel16) + _dot(lo, sel16)


def _sigmoid(x):
    return 1.0 / (1.0 + jnp.exp(-x))


def _softplus(x):
    return jnp.maximum(x, 0.0) + jnp.log1p(jnp.exp(-jnp.abs(x)))


def _rms(x, w):
    ms = jnp.mean(x * x, axis=-1, keepdims=True)
    return x * lax.rsqrt(ms + EPS) * w


def _in_proj_kernel(h_ref, nw_ref, w_ref, wab_ref, wabt_ref, proj_ref, ab_ref, abt_ref, hn_ref):
    @pl.when(pl.program_id(1) == 0)
    def _():
        hn = _rms(h_ref[...], nw_ref[...]).astype(BF16)
        hn_ref[...] = hn
        ab_ref[...] = _dot(hn, wab_ref[...])
        abt_ref[...] = _dot_nt(wabt_ref[...], hn)

    proj_ref[...] = _dot(hn_ref[...], w_ref[...])


def _in_proj(h2d, norm_w, w_main, w_ab, w_abt):
    n, d = h2d.shape
    cols = w_main.shape[1]
    grid = (n // TOKEN_TILE, cols // PROJ_COL_TILE)
    return pl.pallas_call(
        _in_proj_kernel,
        grid=grid,
        in_specs=[
            pl.BlockSpec((TOKEN_TILE, d), lambda i, j: (i, 0)),
            pl.BlockSpec((1, d), lambda i, j: (0, 0)),
            pl.BlockSpec((d, PROJ_COL_TILE), lambda i, j: (0, j)),
            pl.BlockSpec((d, HEAD_DIM), lambda i, j: (0, 0)),
            pl.BlockSpec((2 * N_HEADS, d), lambda i, j: (0, 0)),
        ],
        out_specs=[
            pl.BlockSpec((TOKEN_TILE, PROJ_COL_TILE), lambda i, j: (i, j)),
            pl.BlockSpec((TOKEN_TILE, HEAD_DIM), lambda i, j: (i, 0)),
            pl.BlockSpec((2 * N_HEADS, TOKEN_TILE), lambda i, j: (0, i)),
        ],
        out_shape=[
            jax.ShapeDtypeStruct((n, cols), F32),
            jax.ShapeDtypeStruct((n, HEAD_DIM), F32),
            jax.ShapeDtypeStruct((2 * N_HEADS, n), F32),
        ],
        scratch_shapes=[pltpu.VMEM((TOKEN_TILE, d), BF16)],
        compiler_params=pltpu.CompilerParams(
            dimension_semantics=("arbitrary", "arbitrary"),
            vmem_limit_bytes=VMEM_LIMIT_BYTES),
        name="in_proj",
    )(h2d, norm_w, w_main, w_ab, w_abt)


def _gdn_kernel(qkv_ref, z_ref, ab_ref, abt_ref, convw_ref, alog_row_ref, dtb_row_ref,
                alog_col_ref, dtb_col_ref, normw_ref, ltri_ref, utri_ref, ones_ref,
                o_ref, xpad_ref, s_ref):
    tb = SEQ_BLOCK
    c = GDN_CHUNK
    n_chunks = tb // c

    @pl.when(pl.program_id(1) == 0)
    def _():
        s_ref[...] = jnp.zeros_like(s_ref)
        xpad_ref[0:CONV_HALO, :] = jnp.zeros((CONV_HALO, xpad_ref.shape[1]), F32)

    x = qkv_ref[0]
    xpad_ref[CONV_HALO:CONV_HALO + tb, :] = x
    convw = convw_ref[...]
    acc = x * convw[CONV_WIDTH - 1:CONV_WIDTH, :]
    for kk in range(CONV_WIDTH - 1):
        start = CONV_HALO - (CONV_WIDTH - 1) + kk
        acc = acc + xpad_ref[start:start + tb, :] * convw[kk:kk + 1, :]
    xpad_ref[0:CONV_HALO, :] = x[tb - CONV_HALO:tb, :]
    y = acc * _sigmoid(acc)

    abv = ab_ref[0]
    g_col = -jnp.exp(alog_row_ref[...]) * _softplus(abv + dtb_row_ref[...])
    beta_all = _sigmoid(abv)
    gc_col_all = _dot_sel_left(ltri_ref[...], g_col)
    tot_col_all = _dot_sel_left(ones_ref[...], g_col)
    g_row = -jnp.exp(alog_col_ref[...]) * _softplus(abt_ref[...] + dtb_col_ref[...])
    gc_row_all = _dot_sel_right(g_row, utri_ref[...])

    row = lax.broadcasted_iota(jnp.int32, (tb, tb), 0)
    col = lax.broadcasted_iota(jnp.int32, (tb, tb), 1)
    same_chunk = (row // c) == (col // c)
    causal = same_chunk & (col <= row)
    strict = same_chunk & (col < row)
    eye = (row == col).astype(F32)

    scale = HEAD_DIM ** -0.5
    normw = normw_ref[...]
    for h in range(N_HEADS):
        lo, hi = h * HEAD_DIM, (h + 1) * HEAD_DIM
        qh = y[:, lo:hi]
        kh = y[:, GROUP_WIDTH + lo:GROUP_WIDTH + hi]
        vh = y[:, 2 * GROUP_WIDTH + lo:2 * GROUP_WIDTH + hi]
        qn = qh * (lax.rsqrt(jnp.sum(qh * qh, axis=-1, keepdims=True) + EPS) * scale)
        kn = kh * lax.rsqrt(jnp.sum(kh * kh, axis=-1, keepdims=True) + EPS)

        gc_col = gc_col_all[:, h:h + 1]
        tot_col = tot_col_all[:, h:h + 1]
        beta = beta_all[:, N_HEADS + h:N_HEADS + h + 1]
        gc_row = gc_row_all[h:h + 1, :]
        decay = jnp.exp(jnp.minimum(gc_col - gc_row, 0.0))
        eg = jnp.exp(gc_col)

        kb = kn * beta
        vb = vh * beta
        k16 = kn.astype(BF16)
        lmat = _dot_nt(kb.astype(BF16), k16)
        qk = jnp.where(causal, _dot_nt(qn.astype(BF16), k16) * decay, 0.0)
        xm = jnp.where(strict, -(lmat * decay), 0.0)

        tinv = eye + xm
        pw = xm
        n_sq = int(np.log2(c)) - 1
        for _ in range(n_sq):
            pw16 = pw.astype(BF16)
            pw = _dot(pw16, pw16)
            tinv = tinv + _dot(tinv.astype(BF16), pw.astype(BF16))

        rhs = jnp.concatenate([vb, kb * eg], axis=1).astype(BF16)
        uw = _dot(tinv.astype(BF16), rhs)
        u = uw[:, :HEAD_DIM]
        w = uw[:, HEAD_DIM:]
        qg = (qn * eg).astype(BF16)
        kt = (kn * jnp.exp(tot_col - gc_col)).astype(BF16)
        w16 = w.astype(BF16)
        qk16 = qk.astype(BF16)

        s = s_ref[h]
        outs = []
        for ci in range(n_chunks):
            r0, r1 = ci * c, (ci + 1) * c
            s16 = s.astype(BF16)
            v_new = u[r0:r1] - _dot(w16[r0:r1], s16)
            vn16 = v_new.astype(BF16)
            outs.append(_dot(qg[r0:r1], s16) + _dot(qk16[r0:r1, r0:r1], vn16))
            cd = jnp.exp(tot_col[r0:r0 + 1, :])
            s = s * cd + _dot_tn(kt[r0:r1], vn16)
        s_ref[h] = s
        o = jnp.concatenate(outs, axis=0)

        zz = z_ref[0][:, lo:hi]
        on = o * lax.rsqrt(jnp.mean(o * o, axis=-1, keepdims=True) + EPS)
        o_ref[0, :, lo:hi] = (on * normw * (zz * _sigmoid(zz))).astype(o_ref.dtype)


def _gdn(proj3, ab3, abt, convw, alog_row, dtb_row, alog_col, dtb_col, normw):
    b, t, _ = proj3.shape
    nt = t // SEQ_BLOCK
    tb = SEQ_BLOCK
    c = GDN_CHUNK
    idx = np.arange(tb)
    same = (idx[:, None] // c) == (idx[None, :] // c)
    ltri = jnp.asarray(same & (idx[None, :] <= idx[:, None]), BF16)
    utri = jnp.asarray(same & (idx[:, None] <= idx[None, :]), BF16)
    ones = jnp.asarray(same, BF16)
    const = lambda shape: pl.BlockSpec(shape, lambda bi, ti: (0,) * len(shape))
    return pl.pallas_call(
        _gdn_kernel,
        grid=(b, nt),
        in_specs=[
            pl.BlockSpec((1, tb, 3 * GROUP_WIDTH), lambda bi, ti: (bi, ti, 0)),
            pl.BlockSpec((1, tb, GROUP_WIDTH), lambda bi, ti: (bi, ti, 3)),
            pl.BlockSpec((1, tb, HEAD_DIM), lambda bi, ti: (bi, ti, 0)),
            pl.BlockSpec((2 * N_HEADS, tb), lambda bi, ti: (0, bi * nt + ti)),
            const((CONV_WIDTH, 3 * GROUP_WIDTH)),
            const((1, HEAD_DIM)), const((1, HEAD_DIM)),
            const((2 * N_HEADS, 1)), const((2 * N_HEADS, 1)),
            const((1, HEAD_DIM)),
            const((tb, tb)), const((tb, tb)), const((tb, tb)),
        ],
        out_specs=pl.BlockSpec((1, tb, GROUP_WIDTH), lambda bi, ti: (bi, ti, 0)),
        out_shape=jax.ShapeDtypeStruct((b, t, GROUP_WIDTH), BF16),
        scratch_shapes=[
            pltpu.VMEM((CONV_HALO + tb, 3 * GROUP_WIDTH), F32),
            pltpu.VMEM((N_HEADS, HEAD_DIM, HEAD_DIM), F32),
        ],
        compiler_params=pltpu.CompilerParams(
            dimension_semantics=("arbitrary", "arbitrary"),
            vmem_limit_bytes=VMEM_LIMIT_BYTES),
        name="gdn",
    )(proj3, proj3, ab3, abt, convw, alog_row, dtb_row, alog_col, dtb_col, normw, ltri, utri, ones)


def _hgrn_levels():
    levels = []
    m = SEQ_BLOCK // 2
    while m >= DIAG_BLOCK:
        levels.append(m)
        m //= 2
    return levels


def _hgrn_kernel(layer, f_ref, i_ref, q_ref, g_ref, lbl_ref, normw_ref, ltri_ref, lvl_ref,
                 o_ref, s_ref):
    tb = SEQ_BLOCK

    @pl.when(pl.program_id(1) == 0)
    def _():
        s_ref[...] = jnp.zeros_like(s_ref)

    logits = lbl_ref[...]
    ex = jnp.exp(logits - jnp.max(logits, axis=0, keepdims=True))
    sm = ex / jnp.sum(ex, axis=0, keepdims=True)
    lb = jnp.zeros((1, GROUP_WIDTH), F32)
    for l in range(1, layer + 1):
        lb = lb + sm[l:l + 1, :]
    log_lb = jnp.log(lb)

    x = f_ref[0]
    log_sig = jnp.minimum(x, 0.0) - jnp.log1p(jnp.exp(-jnp.abs(x)))
    d = x - log_lb
    log_f = jnp.maximum(log_sig, log_sig - d) + jnp.log1p(jnp.exp(-jnp.abs(d)))
    k_all = (1.0 - lb) / (1.0 + jnp.exp(x))
    b_all = _dot_sel_left(ltri_ref[...], log_f)
    qv = q_ref[0]
    q_all = qv * _sigmoid(qv)
    v_all = i_ref[0]
    lvl = lvl_ref[...]
    normw = normw_ref[...]
    levels = _hgrn_levels()
    nblk = tb // DIAG_BLOCK
    sub = lax.broadcasted_iota(jnp.int32, (nblk, DIAG_BLOCK, 1), 1)

    for h in range(N_HEADS):
        lo, hi = h * HEAD_DIM, (h + 1) * HEAD_DIM
        bh = b_all[:, lo:hi]
        qh = q_all[:, lo:hi]
        kh = k_all[:, lo:hi]
        vh = v_all[:, lo:hi]
        v16 = vh.astype(BF16)

        att = jnp.zeros((tb, tb), F32)
        for li, m in enumerate(levels):
            b3 = bh.reshape(tb // (2 * m), 2 * m, HEAD_DIM)
            ref = jnp.broadcast_to(b3[:, m - 1:m, :], b3.shape).reshape(tb, HEAD_DIM)
            e = jnp.exp(-jnp.abs(bh - ref))
            att_m = _dot_nt((qh * e).astype(BF16), (kh * e).astype(BF16))
            att = jnp.where(lvl == li, att_m, att)
        o = _dot(att.astype(BF16), v16)

        q3 = qh.reshape(nblk, DIAG_BLOCK, HEAD_DIM)
        k3 = kh.reshape(nblk, DIAG_BLOCK, HEAD_DIM)
        v3 = vh.reshape(nblk, DIAG_BLOCK, HEAD_DIM)
        b3 = bh.reshape(nblk, DIAG_BLOCK, HEAD_DIM)
        od = jnp.zeros((nblk, DIAG_BLOCK, HEAD_DIM), F32)
        for j in range(DIAG_BLOCK):
            pj = q3 * k3[:, j:j + 1, :] * jnp.exp(jnp.minimum(b3 - b3[:, j:j + 1, :], 0.0))
            sj = jnp.sum(pj, axis=-1, keepdims=True)
            sj = jnp.where(sub >= j, sj, 0.0)
            od = od + sj * v3[:, j:j + 1, :]
        o = o + od.reshape(tb, HEAD_DIM)

        st = s_ref[h]
        o = o + _dot_nt((qh * jnp.exp(bh)).astype(BF16), st.astype(BF16))
        b_last = bh[tb - 1:tb, :]
        kdec = (kh * jnp.exp(b_last - bh)).astype(BF16)
        s_ref[h] = st * jnp.exp(b_last) + _dot_tn(v16, kdec)

        gg = g_ref[0][:, lo:hi]
        on = o * lax.rsqrt(jnp.mean(o * o, axis=-1, keepdims=True) + EPS)
        o_ref[0, :, lo:hi] = (on * normw * (gg * _sigmoid(gg))).astype(o_ref.dtype)


def _hgrn(proj3, lb_logits, normw, layer):
    b, t, _ = proj3.shape
    nt = t // SEQ_BLOCK
    tb = SEQ_BLOCK
    idx = np.arange(tb)
    ltri = jnp.asarray(idx[None, :] <= idx[:, None], BF16)
    lvl_np = np.full((tb, tb), -1, np.int32)
    for li, m in enumerate(_hgrn_levels()):
        same_group = (idx[:, None] // (2 * m)) == (idx[None, :] // (2 * m))
        upper = (idx[:, None] % (2 * m)) >= m
        lower = (idx[None, :] % (2 * m)) < m
        lvl_np[same_group & upper & lower] = li
    lvl = jnp.asarray(lvl_np)
    depth = lb_logits.shape[0]
    const = lambda shape: pl.BlockSpec(shape, lambda bi, ti: (0,) * len(shape))
    col_blk = lambda cb: pl.BlockSpec((1, tb, GROUP_WIDTH), lambda bi, ti: (bi, ti, cb))
    return pl.pallas_call(
        functools.partial(_hgrn_kernel, layer),
        grid=(b, nt),
        in_specs=[col_blk(4), col_blk(5), col_blk(6), col_blk(7),
                  const((depth, GROUP_WIDTH)), const((1, HEAD_DIM)),
                  const((tb, tb)), const((tb, tb))],
        out_specs=pl.BlockSpec((1, tb, GROUP_WIDTH), lambda bi, ti: (bi, ti, 0)),
        out_shape=jax.ShapeDtypeStruct((b, t, GROUP_WIDTH), BF16),
        scratch_shapes=[pltpu.VMEM((N_HEADS, HEAD_DIM, HEAD_DIM), F32)],
        compiler_params=pltpu.CompilerParams(
            dimension_semantics=("arbitrary", "arbitrary"),
            vmem_limit_bytes=VMEM_LIMIT_BYTES),
        name="hgrn",
    )(proj3, proj3, proj3, proj3, lb_logits, normw, ltri, lvl)


def _post_mix_kernel(h_ref, ma_ref, mb_ref, p_ref, wout_ref, nw_ref, wup_ref, wdown_ref,
                     wgate_ref, wple_ref, out_ref, h1_ref, hn_ref, acc_ref):
    j = pl.program_id(1)
    nw = nw_ref[...]

    @pl.when(j == 0)
    def _():
        mo = (_dot(ma_ref[...], wout_ref[0:GROUP_WIDTH, :])
              + _dot(mb_ref[...], wout_ref[GROUP_WIDTH:2 * GROUP_WIDTH, :]))
        h1 = h_ref[...] + _rms(mo, nw[0:1, :])
        h1_ref[...] = h1
        hn_ref[...] = _rms(h1, nw[1:2, :]).astype(BF16)
        acc_ref[...] = jnp.zeros_like(acc_ref)

    u = _dot(hn_ref[...], wup_ref[...])
    a = jnp.square(jnp.maximum(u, 0.0)).astype(BF16)
    acc_ref[...] += _dot(a, wdown_ref[...])

    @pl.when(j == pl.num_programs(1) - 1)
    def _():
        h2 = h1_ref[...] + _rms(acc_ref[...], nw[2:3, :])
        e = _rms(_dot(p_ref[...].astype(BF16), wple_ref[...]), nw[3:4, :])
        gate = _sigmoid(_dot(h2.astype(BF16), wgate_ref[...]))
        out_ref[...] = h2 + e * gate


def _post_mix(h2d, mix_a, mix_b, p2d, w_out, norms, w_up, w_down, w_gate, w_ple):
    n, d = h2d.shape
    ff = w_up.shape[1]
    pd = p2d.shape[1]
    grid = (n // TOKEN_TILE, ff // FF_TILE)
    tile = lambda width: pl.BlockSpec((TOKEN_TILE, width), lambda i, j: (i, 0))
    const = lambda shape: pl.BlockSpec(shape, lambda i, j: (0, 0))
    return pl.pallas_call(
        _post_mix_kernel,
        grid=grid,
        in_specs=[
            tile(d), tile(GROUP_WIDTH), tile(GROUP_WIDTH), tile(pd),
            const((d, d)), const((4, d)),
            pl.BlockSpec((d, FF_TILE), lambda i, j: (0, j)),
            pl.BlockSpec((FF_TILE, d), lambda i, j: (j, 0)),
            const((d, d)), const((pd, d)),
        ],
        out_specs=tile(d),
        out_shape=jax.ShapeDtypeStruct((n, d), F32),
        scratch_shapes=[
            pltpu.VMEM((TOKEN_TILE, d), F32),
            pltpu.VMEM((TOKEN_TILE, d), BF16),
            pltpu.VMEM((TOKEN_TILE, d), F32),
        ],
        compiler_params=pltpu.CompilerParams(
            dimension_semantics=("arbitrary", "arbitrary"),
            vmem_limit_bytes=VMEM_LIMIT_BYTES),
        name="post_mix",
    )(h2d, mix_a, mix_b, p2d, w_out, norms, w_up, w_down, w_gate, w_ple)


def kernel(x, p, pre_mix_norm, w_in, gdn_conv, gdn_a_log, gdn_dt_bias, gdn_norm, hgrn_lb_logits,
           hgrn_norm, w_out, post_mix_norm, pre_mlp_norm, w_mlp_up, w_mlp_down, post_mlp_norm,
           w_ple_proj, ple_norm, w_ple_gate):
    bsz, t_len, d = x.shape
    depth = w_in.shape[0]
    n = bsz * t_len
    main_cols = 4 * GROUP_WIDTH
    ab_off = main_cols
    hg_off = ab_off + 2 * N_HEADS
    assert w_in.shape[2] == hg_off + 4 * GROUP_WIDTH
    assert t_len % SEQ_BLOCK == 0 and n % TOKEN_TILE == 0

    pad_row = lambda v: jnp.pad(v.astype(F32), (0, HEAD_DIM - v.shape[0]))[None, :]
    pad_col = lambda v: jnp.pad(v.astype(F32), (0, 2 * N_HEADS - v.shape[0]))[:, None]

    h = x.reshape(n, d)
    for i in range(depth):
        wi = w_in[i]
        w_main = jnp.concatenate([wi[:, :main_cols], wi[:, hg_off:]], axis=1).astype(BF16)
        w_ab_cols = wi[:, ab_off:hg_off]
        w_ab = jnp.pad(w_ab_cols, ((0, 0), (0, HEAD_DIM - 2 * N_HEADS))).astype(BF16)
        w_abt = w_ab_cols.T.astype(BF16)

        proj, ab, abt = _in_proj(h, pre_mix_norm[i][None, :], w_main, w_ab, w_abt)
        proj3 = proj.reshape(bsz, t_len, proj.shape[1])
        ab3 = ab.reshape(bsz, t_len, HEAD_DIM)

        mix_a = _gdn(proj3, ab3, abt, gdn_conv[i].astype(F32),
                     pad_row(gdn_a_log[i]), pad_row(gdn_dt_bias[i]),
                     pad_col(gdn_a_log[i]), pad_col(gdn_dt_bias[i]),
                     gdn_norm[i][None, :].astype(F32))
        mix_b = _hgrn(proj3, hgrn_lb_logits.astype(F32), hgrn_norm[i][None, :].astype(F32), i)

        norms = jnp.stack([post_mix_norm[i], pre_mlp_norm[i], post_mlp_norm[i], ple_norm[i]]).astype(F32)
        h = _post_mix(h, mix_a.reshape(n, GROUP_WIDTH), mix_b.reshape(n, GROUP_WIDTH),
                      p[i].reshape(n, p.shape[-1]),
                      w_out[i].astype(BF16), norms, w_mlp_up[i].astype(BF16),
                      w_mlp_down[i].astype(BF16), w_ple_gate[i].astype(BF16),
                      w_ple_proj[i].astype(BF16))
    return h.reshape(bsz, t_len, d)
```

```python
import functools

import numpy as np
import jax
import jax.numpy as jnp
from jax import lax
from jax.experimental import pallas as pl
from jax.experimental.pallas import tpu as pltpu

F32 = jnp.float32
BF16 = jnp.bfloat16
EPS = 1e-6
LOG2E = float(np.log2(np.e))

HEAD_DIM = 128
N_HEADS = 4
GROUP_WIDTH = N_HEADS * HEAD_DIM
CONV_WIDTH = 4
GDN_CHUNK = 64
SEQ_BLOCK = 256
DIAG_BLOCK = 8
CONV_HALO = 8

TOKEN_TILE = 512
PROJ_COL_TILE = 1024
FF_TILE = 1024
POST_SUB_TILE = 256
VMEM_LIMIT_BYTES = 56 * 1024 * 1024


def _dot(a, b):
    return jnp.dot(a, b, preferred_element_type=F32)


def _dot_nt(a, b):
    return lax.dot_general(a, b, (((1,), (1,)), ((), ())), preferred_element_type=F32)


def _dot_tn(a, b):
    return lax.dot_general(a, b, (((0,), (0,)), ((), ())), preferred_element_type=F32)


def _split3(x):
    hi = x.astype(BF16)
    r1 = x - hi.astype(F32)
    mid = r1.astype(BF16)
    r2 = r1 - mid.astype(F32)
    return hi, mid, r2.astype(BF16)


def _dot_sel_left(sel16, x):
    hi, mid, lo = _split3(x)
    return _dot(sel16, hi) + _dot(sel16, mid) + _dot(sel16, lo)


def _dot_sel_right(x, sel16):
    hi, mid, lo = _split3(x)
    return _dot(hi, sel16) + _dot(mid, sel16) + _dot(lo, sel16)


def _sigmoid(x):
    return 1.0 / (1.0 + jnp.exp(-x))


def _softplus(x):
    return jnp.maximum(x, 0.0) + jnp.log(1.0 + jnp.exp(-jnp.abs(x)))


def _rms(x, w):
    ms = jnp.mean(x * x, axis=-1, keepdims=True)
    return x * lax.rsqrt(ms + EPS) * w


def _resident(shape):
    return pl.BlockSpec(shape, lambda *_: (0,) * len(shape), pipeline_mode=pl.Buffered(1))


def _in_proj_kernel(h_ref, nw_ref, w_ref, wab_ref, wabt_ref, proj_ref, ab_ref, abt_ref):
    hn = _rms(h_ref[...], nw_ref[...]).astype(BF16)
    ab_ref[...] = _dot(hn, wab_ref[...])
    abt_ref[...] = _dot_nt(wabt_ref[...], hn)
    for j in range(w_ref.shape[1] // PROJ_COL_TILE):
        c0, c1 = j * PROJ_COL_TILE, (j + 1) * PROJ_COL_TILE
        proj_ref[:, c0:c1] = _dot(hn, w_ref[:, c0:c1]).astype(proj_ref.dtype)


def _in_proj(h2d, norm_w, w_main, w_ab, w_abt):
    n, d = h2d.shape
    cols = w_main.shape[1]
    return pl.pallas_call(
        _in_proj_kernel,
        grid=(n // TOKEN_TILE,),
        in_specs=[
            pl.BlockSpec((TOKEN_TILE, d), lambda i: (i, 0)),
            _resident((1, d)),
            _resident((d, cols)),
            _resident((d, HEAD_DIM)),
            _resident((2 * N_HEADS, d)),
        ],
        out_specs=[
            pl.BlockSpec((TOKEN_TILE, cols), lambda i: (i, 0)),
            pl.BlockSpec((TOKEN_TILE, HEAD_DIM), lambda i: (i, 0)),
            pl.BlockSpec((2 * N_HEADS, TOKEN_TILE), lambda i: (0, i)),
        ],
        out_shape=[
            jax.ShapeDtypeStruct((n, cols), BF16),
            jax.ShapeDtypeStruct((n, HEAD_DIM), F32),
            jax.ShapeDtypeStruct((2 * N_HEADS, n), F32),
        ],
        compiler_params=pltpu.CompilerParams(
            dimension_semantics=("arbitrary",),
            vmem_limit_bytes=VMEM_LIMIT_BYTES),
        name="in_proj",
    )(h2d, norm_w, w_main, w_ab, w_abt)


def _gdn_kernel(qkv_ref, z_ref, ab_ref, abt_ref, convw_ref, alog_row_ref, dtb_row_ref,
                alog_col_ref, dtb_col_ref, normw_ref, ltri_ref, utri_ref, ones_ref, shift_ref,
                o_ref, halo_ref, s_ref):
    tb = SEQ_BLOCK
    c = GDN_CHUNK
    n_chunks = tb // c
    heads = range(N_HEADS)

    @pl.when(pl.program_id(1) == 0)
    def _():
        s_ref[...] = jnp.zeros_like(s_ref)
        halo_ref[...] = jnp.zeros_like(halo_ref)

    xb = qkv_ref[0]
    x = xb.astype(F32)
    convw = convw_ref[...]
    shifted = _dot(shift_ref[...], xb)
    halo = halo_ref[...]
    sub = lax.broadcasted_iota(jnp.int32, (CONV_HALO, 1), 0)
    acc = x * convw[CONV_WIDTH - 1:CONV_WIDTH, :]
    head_fix = jnp.zeros_like(halo)
    for sft in range(1, CONV_WIDTH):
        wk = convw[CONV_WIDTH - 1 - sft:CONV_WIDTH - sft, :]
        acc = acc + shifted[(sft - 1) * tb:sft * tb, :] * wk
        head_fix = head_fix + jnp.where(sub < sft, pltpu.roll(halo, sft, axis=0), 0.0) * wk
    acc = jnp.concatenate([acc[0:CONV_HALO] + head_fix, acc[CONV_HALO:]], axis=0)
    halo_ref[...] = x[tb - CONV_HALO:tb, :]
    y = acc * _sigmoid(acc)

    abv = ab_ref[0]
    g_col = -jnp.exp(alog_row_ref[...]) * _softplus(abv + dtb_row_ref[...])
    beta_all = _sigmoid(abv)
    gc_col_all = _dot_sel_left(ltri_ref[...], g_col)
    tot_col_all = _dot_sel_left(ones_ref[...], g_col)
    g_row = -jnp.exp(alog_col_ref[...]) * _softplus(abt_ref[...] + dtb_col_ref[...])
    gc_row_all = _dot_sel_right(g_row, utri_ref[...])

    row = lax.broadcasted_iota(jnp.int32, (tb, tb), 0)
    col = lax.broadcasted_iota(jnp.int32, (tb, tb), 1)
    same_chunk = (row // c) == (col // c)
    causal = same_chunk & (col <= row)
    strict = same_chunk & (col < row)
    eye = (row == col).astype(F32)
    scale = HEAD_DIM ** -0.5

    xm, qk16, rhs, qg, kt, tot = [], [], [], [], [], []
    for h in heads:
        lo, hi = h * HEAD_DIM, (h + 1) * HEAD_DIM
        qh = y[:, lo:hi]
        kh = y[:, GROUP_WIDTH + lo:GROUP_WIDTH + hi]
        vh = y[:, 2 * GROUP_WIDTH + lo:2 * GROUP_WIDTH + hi]
        qn = qh * (lax.rsqrt(jnp.sum(qh * qh, axis=-1, keepdims=True) + EPS) * scale)
        kn = kh * lax.rsqrt(jnp.sum(kh * kh, axis=-1, keepdims=True) + EPS)
        gc_col = gc_col_all[:, h:h + 1]
        tot_col = tot_col_all[:, h:h + 1]
        beta = beta_all[:, N_HEADS + h:N_HEADS + h + 1]
        gc_row = gc_row_all[h:h + 1, :]
        decay = jnp.exp(jnp.minimum(gc_col - gc_row, 0.0))
        eg = jnp.exp(gc_col)
        kb = kn * beta
        k16 = kn.astype(BF16)
        xm.append(jnp.where(strict, -(_dot_nt(kb.astype(BF16), k16) * decay), 0.0))
        qk16.append(jnp.where(causal, _dot_nt(qn.astype(BF16), k16) * decay, 0.0).astype(BF16))
        rhs.append(jnp.concatenate([vh * beta, kb * eg], axis=1).astype(BF16))
        qg.append(qn * eg)
        kt.append((kn * jnp.exp(tot_col - gc_col)).astype(BF16))
        tot.append(tot_col)

    n_sq = int(np.log2(c)) - 1
    tinv = [eye + xm[h] for h in heads]
    pw = [_dot(xm[h].astype(BF16), xm[h].astype(BF16)) for h in heads]
    for step in range(n_sq):
        pw16 = [pw[h].astype(BF16) for h in heads]
        tinv = [tinv[h] + _dot(tinv[h].astype(BF16), pw16[h]) for h in heads]
        if step + 1 < n_sq:
            pw = [_dot(pw16[h], pw16[h]) for h in heads]
    uw16 = [_dot(tinv[h].astype(BF16), rhs[h]).astype(BF16) for h in heads]

    ns16 = [[None] * n_chunks for _ in heads]
    bs = [[None] * n_chunks for _ in heads]
    gs16 = [[None] * n_chunks for _ in heads]
    hs = [[None] * n_chunks for _ in heads]
    for ci in range(n_chunks):
        r0, r1 = ci * c, (ci + 1) * c
        for h in heads:
            kuw = _dot_tn(kt[h][r0:r1], uw16[h][r0:r1])
            quw = _dot(qk16[h][r0:r1, r0:r1], uw16[h][r0:r1])
            bs[h][ci] = kuw[:, :HEAD_DIM]
            ns16[h][ci] = kuw[:, HEAD_DIM:].astype(BF16)
            hs[h][ci] = quw[:, :HEAD_DIM]
            gs16[h][ci] = (qg[h][r0:r1] - quw[:, HEAD_DIM:]).astype(BF16)

    s = [s_ref[h] for h in heads]
    outs = [[] for _ in heads]
    for ci in range(n_chunks):
        r0 = ci * c
        for h in heads:
            s16 = s[h].astype(BF16)
            outs[h].append(hs[h][ci] + _dot(gs16[h][ci], s16))
            cd = jnp.exp(tot[h][r0:r0 + 1, :])
            s[h] = s[h] * cd + bs[h][ci] - _dot(ns16[h][ci], s16)

    normw = normw_ref[...]
    for h in heads:
        lo, hi = h * HEAD_DIM, (h + 1) * HEAD_DIM
        s_ref[h] = s[h]
        o = jnp.concatenate(outs[h], axis=0)
        zz = z_ref[0][:, lo:hi].astype(F32)
        on = o * lax.rsqrt(jnp.mean(o * o, axis=-1, keepdims=True) + EPS)
        o_ref[0, :, lo:hi] = (on * normw * (zz * _sigmoid(zz))).astype(o_ref.dtype)


def _gdn(proj3, ab3, abt, convw, alog_row, dtb_row, alog_col, dtb_col, normw):
    b, t, _ = proj3.shape
    nt = t // SEQ_BLOCK
    tb = SEQ_BLOCK
    c = GDN_CHUNK
    idx = np.arange(tb)
    same = (idx[:, None] // c) == (idx[None, :] // c)
    ltri = jnp.asarray(same & (idx[None, :] <= idx[:, None]), BF16)
    utri = jnp.asarray(same & (idx[:, None] <= idx[None, :]), BF16)
    ones = jnp.asarray(same, BF16)
    shift = jnp.asarray(np.concatenate(
        [idx[None, :] == idx[:, None] - sft for sft in range(1, CONV_WIDTH)], axis=0), BF16)
    return pl.pallas_call(
        _gdn_kernel,
        grid=(b, nt),
        in_specs=[
            pl.BlockSpec((1, tb, 3 * GROUP_WIDTH), lambda bi, ti: (bi, ti, 0)),
            pl.BlockSpec((1, tb, GROUP_WIDTH), lambda bi, ti: (bi, ti, 3)),
            pl.BlockSpec((1, tb, HEAD_DIM), lambda bi, ti: (bi, ti, 0)),
            pl.BlockSpec((2 * N_HEADS, tb), lambda bi, ti: (0, bi * nt + ti)),
            _resident((CONV_WIDTH, 3 * GROUP_WIDTH)),
            _resident((1, HEAD_DIM)), _resident((1, HEAD_DIM)),
            _resident((2 * N_HEADS, 1)), _resident((2 * N_HEADS, 1)),
            _resident((1, HEAD_DIM)),
            _resident((tb, tb)), _resident((tb, tb)), _resident((tb, tb)),
            _resident(((CONV_WIDTH - 1) * tb, tb)),
        ],
        out_specs=pl.BlockSpec((1, tb, GROUP_WIDTH), lambda bi, ti: (bi, ti, 0)),
        out_shape=jax.ShapeDtypeStruct((b, t, GROUP_WIDTH), BF16),
        scratch_shapes=[
            pltpu.VMEM((CONV_HALO, 3 * GROUP_WIDTH), F32),
            pltpu.VMEM((N_HEADS, HEAD_DIM, HEAD_DIM), F32),
        ],
        compiler_params=pltpu.CompilerParams(
            dimension_semantics=("arbitrary", "arbitrary"),
            vmem_limit_bytes=VMEM_LIMIT_BYTES),
        name="gdn",
    )(proj3, proj3, ab3, abt, convw, alog_row, dtb_row, alog_col, dtb_col, normw, ltri, utri, ones,
      shift)


def _hgrn_levels():
    levels = []
    m = SEQ_BLOCK // 2
    while m >= DIAG_BLOCK:
        levels.append(m)
        m //= 2
    return levels


def _hgrn_kernel(layer, f_ref, i_ref, q_ref, g_ref, lbl_ref, normw_ref, ltri_ref, lvl_ref,
                 o_ref, s_ref):
    tb = SEQ_BLOCK

    @pl.when(pl.program_id(1) == 0)
    def _():
        s_ref[...] = jnp.zeros_like(s_ref)

    logits = lbl_ref[...]
    ex = jnp.exp(logits - jnp.max(logits, axis=0, keepdims=True))
    sm = ex / jnp.sum(ex, axis=0, keepdims=True)
    lb = jnp.zeros((1, GROUP_WIDTH), F32)
    for l in range(1, layer + 1):
        lb = lb + sm[l:l + 1, :]
    log_lb = jnp.log(lb)

    x = f_ref[0].astype(F32)
    t = jnp.exp(-jnp.abs(x))
    r = 1.0 / (1.0 + t)
    log_sig = jnp.minimum(x, 0.0) + jnp.log(r)
    d = x - log_lb
    log_f = jnp.maximum(log_sig, log_sig - d) + jnp.log(1.0 + jnp.exp(-jnp.abs(d)))
    k_all = (1.0 - lb) * (r * jnp.where(x >= 0.0, t, 1.0))
    b2_all = _dot_sel_left(ltri_ref[...], log_f * LOG2E)
    qv = q_ref[0].astype(F32)
    q_all = qv * _sigmoid(qv)
    v_all = i_ref[0].astype(F32)
    lvl = lvl_ref[...]
    normw = normw_ref[...]
    levels = _hgrn_levels()
    nblk = tb // DIAG_BLOCK
    sub = lax.broadcasted_iota(jnp.int32, (nblk, DIAG_BLOCK, 1), 1)
    neg_inf = jnp.float32(-jnp.inf)

    for h in range(N_HEADS):
        lo, hi = h * HEAD_DIM, (h + 1) * HEAD_DIM
        bh = b2_all[:, lo:hi]
        qh = q_all[:, lo:hi]
        kh = k_all[:, lo:hi]
        vh = v_all[:, lo:hi]
        v16 = vh.astype(BF16)

        att = jnp.zeros((tb, tb), F32)
        for li, m in enumerate(levels):
            b3 = bh.reshape(tb // (2 * m), 2 * m, HEAD_DIM)
            ref = jnp.broadcast_to(b3[:, m - 1:m, :], b3.shape).reshape(tb, HEAD_DIM)
            e = jnp.exp2(-jnp.abs(bh - ref))
            att_m = _dot_nt((qh * e).astype(BF16), (kh * e).astype(BF16))
            att = jnp.where(lvl == li, att_m, att)
        o = _dot(att.astype(BF16), v16)

        q3 = qh.reshape(nblk, DIAG_BLOCK, HEAD_DIM)
        k3 = kh.reshape(nblk, DIAG_BLOCK, HEAD_DIM)
        v3 = vh.reshape(nblk, DIAG_BLOCK, HEAD_DIM)
        b3 = bh.reshape(nblk, DIAG_BLOCK, HEAD_DIM)
        od = jnp.zeros((nblk, DIAG_BLOCK, HEAD_DIM), F32)
        for j in range(DIAG_BLOCK):
            dj = jnp.where(sub >= j, b3 - b3[:, j:j + 1, :], neg_inf)
            pj = (q3 * k3[:, j:j + 1, :]) * jnp.exp2(dj)
            od = od + jnp.sum(pj, axis=-1, keepdims=True) * v3[:, j:j + 1, :]
        o = o + od.reshape(tb, HEAD_DIM)

        st = s_ref[h]
        o = o + _dot_nt((qh * jnp.exp2(bh)).astype(BF16), st.astype(BF16))
        b_last = bh[tb - 1:tb, :]
        kdec = (kh * jnp.exp2(b_last - bh)).astype(BF16)
        s_ref[h] = st * jnp.exp2(b_last) + _dot_tn(v16, kdec)

        gg = g_ref[0][:, lo:hi].astype(F32)
        on = o * lax.rsqrt(jnp.mean(o * o, axis=-1, keepdims=True) + EPS)
        o_ref[0, :, lo:hi] = (on * normw * (gg * _sigmoid(gg))).astype(o_ref.dtype)


def _hgrn(proj3, lb_logits, normw, layer):
    b, t, _ = proj3.shape
    nt = t // SEQ_BLOCK
    tb = SEQ_BLOCK
    idx = np.arange(tb)
    ltri = jnp.asarray(idx[None, :] <= idx[:, None], BF16)
    lvl_np = np.full((tb, tb), -1, np.int32)
    for li, m in enumerate(_hgrn_levels()):
        same_group = (idx[:, None] // (2 * m)) == (idx[None, :] // (2 * m))
        upper = (idx[:, None] % (2 * m)) >= m
        lower = (idx[None, :] % (2 * m)) < m
        lvl_np[same_group & upper & lower] = li
    lvl = jnp.asarray(lvl_np)
    depth = lb_logits.shape[0]
    col_blk = lambda cb: pl.BlockSpec((1, tb, GROUP_WIDTH), lambda bi, ti: (bi, ti, cb))
    return pl.pallas_call(
        functools.partial(_hgrn_kernel, layer),
        grid=(b, nt),
        in_specs=[col_blk(4), col_blk(5), col_blk(6), col_blk(7),
                  _resident((depth, GROUP_WIDTH)), _resident((1, HEAD_DIM)),
                  _resident((tb, tb)), _resident((tb, tb))],
        out_specs=pl.BlockSpec((1, tb, GROUP_WIDTH), lambda bi, ti: (bi, ti, 0)),
        out_shape=jax.ShapeDtypeStruct((b, t, GROUP_WIDTH), BF16),
        scratch_shapes=[pltpu.VMEM((N_HEADS, HEAD_DIM, HEAD_DIM), F32)],
        compiler_params=pltpu.CompilerParams(
            dimension_semantics=("arbitrary", "arbitrary"),
            vmem_limit_bytes=VMEM_LIMIT_BYTES),
        name="hgrn",
    )(proj3, proj3, proj3, proj3, lb_logits, normw, ltri, lvl)


def _post_mix_kernel(h_ref, ma_ref, mb_ref, p_ref, wout_ref, nw_ref, wup_ref, wdown_ref,
                     wgate_ref, wple_ref, out_ref):
    nw = nw_ref[...]
    subs = [(r, r + POST_SUB_TILE) for r in range(0, h_ref.shape[0], POST_SUB_TILE)]
    h1s, hns = [], []
    for r0, r1 in subs:
        mo = (_dot(ma_ref[r0:r1, :], wout_ref[0:GROUP_WIDTH, :])
              + _dot(mb_ref[r0:r1, :], wout_ref[GROUP_WIDTH:2 * GROUP_WIDTH, :]))
        h1 = h_ref[r0:r1, :] + _rms(mo, nw[0:1, :])
        h1s.append(h1)
        hns.append(_rms(h1, nw[1:2, :]).astype(BF16))
    accs = [None] * len(subs)
    for j in range(wup_ref.shape[1] // FF_TILE):
        f0, f1 = j * FF_TILE, (j + 1) * FF_TILE
        for si in range(len(subs)):
            u = _dot(hns[si], wup_ref[:, f0:f1])
            a = jnp.square(jnp.maximum(u, 0.0)).astype(BF16)
            part = _dot(a, wdown_ref[f0:f1, :])
            accs[si] = part if accs[si] is None else accs[si] + part
    for si, (r0, r1) in enumerate(subs):
        h2 = h1s[si] + _rms(accs[si], nw[2:3, :])
        e = _rms(_dot(p_ref[r0:r1, :].astype(BF16), wple_ref[...]), nw[3:4, :])
        gate = _sigmoid(_dot(h2.astype(BF16), wgate_ref[...]))
        out_ref[r0:r1, :] = h2 + e * gate


def _post_mix(h2d, mix_a, mix_b, p2d, w_out, norms, w_up, w_down, w_gate, w_ple):
    n, d = h2d.shape
    ff = w_up.shape[1]
    pd = p2d.shape[1]
    tile = lambda width: pl.BlockSpec((TOKEN_TILE, width), lambda i: (i, 0))
    return pl.pallas_call(
        _post_mix_kernel,
        grid=(n // TOKEN_TILE,),
        in_specs=[
            tile(d), tile(GROUP_WIDTH), tile(GROUP_WIDTH), tile(pd),
            _resident((d, d)), _resident((4, d)),
            _resident((d, ff)), _resident((ff, d)),
            _resident((d, d)), _resident((pd, d)),
        ],
        out_specs=tile(d),
        out_shape=jax.ShapeDtypeStruct((n, d), F32),
        compiler_params=pltpu.CompilerParams(
            dimension_semantics=("arbitrary",),
            vmem_limit_bytes=VMEM_LIMIT_BYTES),
        name="post_mix",
    )(h2d, mix_a, mix_b, p2d, w_out, norms, w_up, w_down, w_gate, w_ple)


def kernel(x, p, pre_mix_norm, w_in, gdn_conv, gdn_a_log, gdn_dt_bias, gdn_norm, hgrn_lb_logits,
           hgrn_norm, w_out, post_mix_norm, pre_mlp_norm, w_mlp_up, w_mlp_down, post_mlp_norm,
           w_ple_proj, ple_norm, w_ple_gate):
    bsz, t_len, d = x.shape
    depth = w_in.shape[0]
    n = bsz * t_len
    main_cols = 4 * GROUP_WIDTH
    ab_off = main_cols
    hg_off = ab_off + 2 * N_HEADS
    assert w_in.shape[2] == hg_off + 4 * GROUP_WIDTH
    assert t_len % SEQ_BLOCK == 0 and n % TOKEN_TILE == 0

    pad_row = lambda v: jnp.pad(v.astype(F32), (0, HEAD_DIM - v.shape[0]))[None, :]
    pad_col = lambda v: jnp.pad(v.astype(F32), (0, 2 * N_HEADS - v.shape[0]))[:, None]

    h = x.reshape(n, d)
    for i in range(depth):
        wi = w_in[i]
        w_main = jnp.concatenate([wi[:, :main_cols], wi[:, hg_off:]], axis=1).astype(BF16)
        w_ab_cols = wi[:, ab_off:hg_off]
        w_ab = jnp.pad(w_ab_cols, ((0, 0), (0, HEAD_DIM - 2 * N_HEADS))).astype(BF16)
        w_abt = w_ab_cols.T.astype(BF16)

        proj, ab, abt = _in_proj(h, pre_mix_norm[i][None, :], w_main, w_ab, w_abt)
        proj3 = proj.reshape(bsz, t_len, proj.shape[1])
        ab3 = ab.reshape(bsz, t_len, HEAD_DIM)

        mix_a = _gdn(proj3, ab3, abt, gdn_conv[i].astype(F32),
                     pad_row(gdn_a_log[i]), pad_row(gdn_dt_bias[i]),
                     pad_col(gdn_a_log[i]), pad_col(gdn_dt_bias[i]),
                     gdn_norm[i][None, :].astype(F32))
        mix_b = _hgrn(proj3, hgrn_lb_logits.astype(F32), hgrn_norm[i][None, :].astype(F32), i)

        norms = jnp.stack([post_mix_norm[i], pre_mlp_norm[i], post_mlp_norm[i], ple_norm[i]]).astype(F32)
        h = _post_mix(h, mix_a.reshape(n, GROUP_WIDTH), mix_b.reshape(n, GROUP_WIDTH),
                      p[i].reshape(n, p.shape[-1]),
                      w_out[i].astype(BF16), norms, w_mlp_up[i].astype(BF16),
                      w_mlp_down[i].astype(BF16), w_ple_gate[i].astype(BF16),
                      w_ple_proj[i].astype(BF16))
    return h.reshape(bsz, t_len, d)
```

```python
import functools

import numpy as np
import jax
import jax.numpy as jnp
from jax import lax
from jax.experimental import pallas as pl
from jax.experimental.pallas import tpu as pltpu

F32 = jnp.float32
BF16 = jnp.bfloat16
EPS = 1e-6
LOG2E = float(np.log2(np.e))

HEAD_DIM = 128
N_HEADS = 4
GROUP_WIDTH = N_HEADS * HEAD_DIM
CONV_WIDTH = 4
GDN_CHUNK = 64
SEQ_BLOCK = 256
DIAG_BLOCK = 8
CONV_HALO = 8

TOKEN_TILE = 512
PROJ_COL_TILE = 1024
FF_TILE = 1024
POST_SUB_TILE = 256
VMEM_LIMIT_BYTES = 56 * 1024 * 1024


def _dot(a, b):
    return jnp.dot(a, b, preferred_element_type=F32)


def _dot_nt(a, b):
    return lax.dot_general(a, b, (((1,), (1,)), ((), ())), preferred_element_type=F32)


def _dot_tn(a, b):
    return lax.dot_general(a, b, (((0,), (0,)), ((), ())), preferred_element_type=F32)


def _split3(x):
    hi = x.astype(BF16)
    r1 = x - hi.astype(F32)
    mid = r1.astype(BF16)
    r2 = r1 - mid.astype(F32)
    return hi, mid, r2.astype(BF16)


def _dot_sel_left(sel16, x):
    hi, mid, lo = _split3(x)
    return _dot(sel16, hi) + _dot(sel16, mid) + _dot(sel16, lo)


def _dot_sel_right(x, sel16):
    hi, mid, lo = _split3(x)
    return _dot(hi, sel16) + _dot(mid, sel16) + _dot(lo, sel16)


def _sigmoid(x):
    return 1.0 / (1.0 + jnp.exp(-x))


def _softplus(x):
    return jnp.maximum(x, 0.0) + jnp.log(1.0 + jnp.exp(-jnp.abs(x)))


def _rms(x, w):
    ms = jnp.mean(x * x, axis=-1, keepdims=True)
    return x * lax.rsqrt(ms + EPS) * w


def _resident(shape):
    return pl.BlockSpec(shape, lambda *_: (0,) * len(shape), pipeline_mode=pl.Buffered(1))


def _in_proj_kernel(h_ref, nw_ref, w_ref, wab_ref, wabt_ref, proj_ref, ab_ref, abt_ref):
    hn = _rms(h_ref[...], nw_ref[...]).astype(BF16)
    ab_ref[...] = _dot(hn, wab_ref[...])
    abt_ref[...] = _dot_nt(wabt_ref[...], hn)
    for j in range(w_ref.shape[1] // PROJ_COL_TILE):
        c0, c1 = j * PROJ_COL_TILE, (j + 1) * PROJ_COL_TILE
        proj_ref[:, c0:c1] = _dot(hn, w_ref[:, c0:c1]).astype(proj_ref.dtype)


def _in_proj(h2d, norm_w, w_main, w_ab, w_abt):
    n, d = h2d.shape
    cols = w_main.shape[1]
    return pl.pallas_call(
        _in_proj_kernel,
        grid=(n // TOKEN_TILE,),
        in_specs=[
            pl.BlockSpec((TOKEN_TILE, d), lambda i: (i, 0)),
            _resident((1, d)),
            _resident((d, cols)),
            _resident((d, HEAD_DIM)),
            _resident((2 * N_HEADS, d)),
        ],
        out_specs=[
            pl.BlockSpec((TOKEN_TILE, cols), lambda i: (i, 0)),
            pl.BlockSpec((TOKEN_TILE, HEAD_DIM), lambda i: (i, 0)),
            pl.BlockSpec((2 * N_HEADS, TOKEN_TILE), lambda i: (0, i)),
        ],
        out_shape=[
            jax.ShapeDtypeStruct((n, cols), BF16),
            jax.ShapeDtypeStruct((n, HEAD_DIM), F32),
            jax.ShapeDtypeStruct((2 * N_HEADS, n), F32),
        ],
        compiler_params=pltpu.CompilerParams(
            dimension_semantics=("arbitrary",),
            vmem_limit_bytes=VMEM_LIMIT_BYTES),
        name="in_proj",
    )(h2d, norm_w, w_main, w_ab, w_abt)


def _gdn_body(qkv_ref, z_ref, ab_ref, abt_ref, convw_ref, alog_row_ref, dtb_row_ref,
              alog_col_ref, dtb_col_ref, normw_ref, ltri_ref, utri_ref, ones_ref, shift_ref,
              o_ref, halo_ref, s_ref):
    tb = SEQ_BLOCK
    c = GDN_CHUNK
    n_chunks = tb // c
    heads = range(N_HEADS)

    xb = qkv_ref[0]
    x = xb.astype(F32)
    convw = convw_ref[...]
    shifted = _dot(shift_ref[...], xb)
    halo = halo_ref[...]
    sub = lax.broadcasted_iota(jnp.int32, (CONV_HALO, 1), 0)
    acc = x * convw[CONV_WIDTH - 1:CONV_WIDTH, :]
    head_fix = jnp.zeros_like(halo)
    for sft in range(1, CONV_WIDTH):
        wk = convw[CONV_WIDTH - 1 - sft:CONV_WIDTH - sft, :]
        acc = acc + shifted[(sft - 1) * tb:sft * tb, :] * wk
        head_fix = head_fix + jnp.where(sub < sft, pltpu.roll(halo, sft, axis=0), 0.0) * wk
    acc = jnp.concatenate([acc[0:CONV_HALO] + head_fix, acc[CONV_HALO:]], axis=0)
    halo_ref[...] = x[tb - CONV_HALO:tb, :]
    y = acc * _sigmoid(acc)
    yield

    abv = ab_ref[0]
    g_col = -jnp.exp(alog_row_ref[...]) * _softplus(abv + dtb_row_ref[...])
    beta_all = _sigmoid(abv)
    gc_col_all = _dot_sel_left(ltri_ref[...], g_col)
    tot_col_all = _dot_sel_left(ones_ref[...], g_col)
    g_row = -jnp.exp(alog_col_ref[...]) * _softplus(abt_ref[...] + dtb_col_ref[...])
    gc_row_all = _dot_sel_right(g_row, utri_ref[...])

    row = lax.broadcasted_iota(jnp.int32, (tb, tb), 0)
    col = lax.broadcasted_iota(jnp.int32, (tb, tb), 1)
    same_chunk = (row // c) == (col // c)
    causal = same_chunk & (col <= row)
    strict = same_chunk & (col < row)
    eye = (row == col).astype(F32)
    scale = HEAD_DIM ** -0.5

    xm, qk16, rhs, qg, kt, tot = [], [], [], [], [], []
    for h in heads:
        lo, hi = h * HEAD_DIM, (h + 1) * HEAD_DIM
        qh = y[:, lo:hi]
        kh = y[:, GROUP_WIDTH + lo:GROUP_WIDTH + hi]
        vh = y[:, 2 * GROUP_WIDTH + lo:2 * GROUP_WIDTH + hi]
        qn = qh * (lax.rsqrt(jnp.sum(qh * qh, axis=-1, keepdims=True) + EPS) * scale)
        kn = kh * lax.rsqrt(jnp.sum(kh * kh, axis=-1, keepdims=True) + EPS)
        gc_col = gc_col_all[:, h:h + 1]
        tot_col = tot_col_all[:, h:h + 1]
        beta = beta_all[:, N_HEADS + h:N_HEADS + h + 1]
        gc_row = gc_row_all[h:h + 1, :]
        decay = jnp.exp(jnp.minimum(gc_col - gc_row, 0.0))
        eg = jnp.exp(gc_col)
        kb = kn * beta
        k16 = kn.astype(BF16)
        xm.append(jnp.where(strict, -(_dot_nt(kb.astype(BF16), k16) * decay), 0.0))
        qk16.append(jnp.where(causal, _dot_nt(qn.astype(BF16), k16) * decay, 0.0).astype(BF16))
        rhs.append(jnp.concatenate([vh * beta, kb * eg], axis=1).astype(BF16))
        qg.append(qn * eg)
        kt.append((kn * jnp.exp(tot_col - gc_col)).astype(BF16))
        tot.append(tot_col)
        yield

    n_sq = int(np.log2(c)) - 1
    tinv = [eye + xm[h] for h in heads]
    pw = [_dot(xm[h].astype(BF16), xm[h].astype(BF16)) for h in heads]
    for step in range(n_sq):
        pw16 = [pw[h].astype(BF16) for h in heads]
        tinv = [tinv[h] + _dot(tinv[h].astype(BF16), pw16[h]) for h in heads]
        if step + 1 < n_sq:
            pw = [_dot(pw16[h], pw16[h]) for h in heads]
        yield
    uw16 = [_dot(tinv[h].astype(BF16), rhs[h]).astype(BF16) for h in heads]
    yield

    ns16 = [[None] * n_chunks for _ in heads]
    bs = [[None] * n_chunks for _ in heads]
    gs16 = [[None] * n_chunks for _ in heads]
    hs = [[None] * n_chunks for _ in heads]
    for ci in range(n_chunks):
        r0, r1 = ci * c, (ci + 1) * c
        for h in heads:
            kuw = _dot_tn(kt[h][r0:r1], uw16[h][r0:r1])
            quw = _dot(qk16[h][r0:r1, r0:r1], uw16[h][r0:r1])
            bs[h][ci] = kuw[:, :HEAD_DIM]
            ns16[h][ci] = kuw[:, HEAD_DIM:].astype(BF16)
            hs[h][ci] = quw[:, :HEAD_DIM]
            gs16[h][ci] = (qg[h][r0:r1] - quw[:, HEAD_DIM:]).astype(BF16)
        yield

    s = [s_ref[h] for h in heads]
    outs = [[] for _ in heads]
    for ci in range(n_chunks):
        r0 = ci * c
        for h in heads:
            s16 = s[h].astype(BF16)
            outs[h].append(hs[h][ci] + _dot(gs16[h][ci], s16))
            cd = jnp.exp(tot[h][r0:r0 + 1, :])
            s[h] = s[h] * cd + bs[h][ci] - _dot(ns16[h][ci], s16)
        yield

    normw = normw_ref[...]
    for h in heads:
        lo, hi = h * HEAD_DIM, (h + 1) * HEAD_DIM
        s_ref[h] = s[h]
        o = jnp.concatenate(outs[h], axis=0)
        zz = z_ref[0][:, lo:hi].astype(F32)
        on = o * lax.rsqrt(jnp.mean(o * o, axis=-1, keepdims=True) + EPS)
        o_ref[0, :, lo:hi] = (on * normw * (zz * _sigmoid(zz))).astype(o_ref.dtype)


def _hgrn_levels():
    levels = []
    m = SEQ_BLOCK // 2
    while m >= DIAG_BLOCK:
        levels.append(m)
        m //= 2
    return levels


def _hgrn_body(layer, f_ref, i_ref, q_ref, g_ref, lbl_ref, normw_ref, ltri_ref, lvl_ref,
               o_ref, s_ref):
    tb = SEQ_BLOCK

    logits = lbl_ref[...]
    ex = jnp.exp(logits - jnp.max(logits, axis=0, keepdims=True))
    sm = ex / jnp.sum(ex, axis=0, keepdims=True)
    lb = jnp.zeros((1, GROUP_WIDTH), F32)
    for l in range(1, layer + 1):
        lb = lb + sm[l:l + 1, :]
    log_lb = jnp.log(lb)

    x = f_ref[0].astype(F32)
    t = jnp.exp(-jnp.abs(x))
    r = 1.0 / (1.0 + t)
    log_sig = jnp.minimum(x, 0.0) + jnp.log(r)
    d = x - log_lb
    log_f = jnp.maximum(log_sig, log_sig - d) + jnp.log(1.0 + jnp.exp(-jnp.abs(d)))
    k_all = (1.0 - lb) * (r * jnp.where(x >= 0.0, t, 1.0))
    b2_all = _dot_sel_left(ltri_ref[...], log_f * LOG2E)
    qv = q_ref[0].astype(F32)
    q_all = qv * _sigmoid(qv)
    v_all = i_ref[0].astype(F32)
    lvl = lvl_ref[...]
    normw = normw_ref[...]
    levels = _hgrn_levels()
    nblk = tb // DIAG_BLOCK
    sub = lax.broadcasted_iota(jnp.int32, (nblk, DIAG_BLOCK, 1), 1)
    neg_inf = jnp.float32(-jnp.inf)
    yield

    for h in range(N_HEADS):
        lo, hi = h * HEAD_DIM, (h + 1) * HEAD_DIM
        bh = b2_all[:, lo:hi]
        qh = q_all[:, lo:hi]
        kh = k_all[:, lo:hi]
        vh = v_all[:, lo:hi]
        v16 = vh.astype(BF16)

        att = jnp.zeros((tb, tb), F32)
        for li, m in enumerate(levels):
            b3 = bh.reshape(tb // (2 * m), 2 * m, HEAD_DIM)
            ref = jnp.broadcast_to(b3[:, m - 1:m, :], b3.shape).reshape(tb, HEAD_DIM)
            e = jnp.exp2(-jnp.abs(bh - ref))
            att_m = _dot_nt((qh * e).astype(BF16), (kh * e).astype(BF16))
            att = jnp.where(lvl == li, att_m, att)
            yield
        o = _dot(att.astype(BF16), v16)

        q3 = qh.reshape(nblk, DIAG_BLOCK, HEAD_DIM)
        k3 = kh.reshape(nblk, DIAG_BLOCK, HEAD_DIM)
        v3 = vh.reshape(nblk, DIAG_BLOCK, HEAD_DIM)
        b3 = bh.reshape(nblk, DIAG_BLOCK, HEAD_DIM)
        od = jnp.zeros((nblk, DIAG_BLOCK, HEAD_DIM), F32)
        for j in range(DIAG_BLOCK):
            dj = jnp.where(sub >= j, b3 - b3[:, j:j + 1, :], neg_inf)
            pj = (q3 * k3[:, j:j + 1, :]) * jnp.exp2(dj)
            od = od + jnp.sum(pj, axis=-1, keepdims=True) * v3[:, j:j + 1, :]
            yield
        o = o + od.reshape(tb, HEAD_DIM)

        st = s_ref[h]
        o = o + _dot_nt((qh * jnp.exp2(bh)).astype(BF16), st.astype(BF16))
        b_last = bh[tb - 1:tb, :]
        kdec = (kh * jnp.exp2(b_last - bh)).astype(BF16)
        s_ref[h] = st * jnp.exp2(b_last) + _dot_tn(v16, kdec)

        gg = g_ref[0][:, lo:hi].astype(F32)
        on = o * lax.rsqrt(jnp.mean(o * o, axis=-1, keepdims=True) + EPS)
        o_ref[0, :, GROUP_WIDTH + lo:GROUP_WIDTH + hi] = (
            on * normw * (gg * _sigmoid(gg))).astype(o_ref.dtype)
        yield


N_GDN_IN = 14
N_HGRN_IN = 8
GDN_LEAD_SEGMENTS = 5
HGRN_SEGMENTS_PER_GDN = 4


def _mixer_kernel(layer, *refs):
    gdn_in = refs[:N_GDN_IN]
    hgrn_in = refs[N_GDN_IN:N_GDN_IN + N_HGRN_IN]
    o_ref, halo_ref, sg_ref, sh_ref = refs[N_GDN_IN + N_HGRN_IN:]

    @pl.when(pl.program_id(1) == 0)
    def _():
        halo_ref[...] = jnp.zeros_like(halo_ref)
        sg_ref[...] = jnp.zeros_like(sg_ref)
        sh_ref[...] = jnp.zeros_like(sh_ref)

    gdn = _gdn_body(*gdn_in, o_ref, halo_ref, sg_ref)
    hgrn = _hgrn_body(layer, *hgrn_in, o_ref, sh_ref)
    done = object()
    for _ in range(GDN_LEAD_SEGMENTS):
        next(gdn)
    gdn_done = hgrn_done = False
    while not (gdn_done and hgrn_done):
        gdn_done = gdn_done or next(gdn, done) is done
        for _ in range(HGRN_SEGMENTS_PER_GDN):
            hgrn_done = hgrn_done or next(hgrn, done) is done


def _mixer(proj3, ab3, abt, convw, alog_row, dtb_row, alog_col, dtb_col, gdn_normw,
           lb_logits, hgrn_normw, layer):
    b, t, _ = proj3.shape
    nt = t // SEQ_BLOCK
    tb = SEQ_BLOCK
    c = GDN_CHUNK
    idx = np.arange(tb)
    same = (idx[:, None] // c) == (idx[None, :] // c)
    ltri_c = jnp.asarray(same & (idx[None, :] <= idx[:, None]), BF16)
    utri_c = jnp.asarray(same & (idx[:, None] <= idx[None, :]), BF16)
    ones_c = jnp.asarray(same, BF16)
    shift = jnp.asarray(np.concatenate(
        [idx[None, :] == idx[:, None] - sft for sft in range(1, CONV_WIDTH)], axis=0), BF16)
    ltri = jnp.asarray(idx[None, :] <= idx[:, None], BF16)
    lvl_np = np.full((tb, tb), -1, np.int32)
    for li, m in enumerate(_hgrn_levels()):
        same_group = (idx[:, None] // (2 * m)) == (idx[None, :] // (2 * m))
        upper = (idx[:, None] % (2 * m)) >= m
        lower = (idx[None, :] % (2 * m)) < m
        lvl_np[same_group & upper & lower] = li
    lvl = jnp.asarray(lvl_np)
    depth = lb_logits.shape[0]
    col_blk = lambda cb: pl.BlockSpec((1, tb, GROUP_WIDTH), lambda bi, ti: (bi, ti, cb))
    gdn_specs = [
        pl.BlockSpec((1, tb, 3 * GROUP_WIDTH), lambda bi, ti: (bi, ti, 0)),
        col_blk(3),
        pl.BlockSpec((1, tb, HEAD_DIM), lambda bi, ti: (bi, ti, 0)),
        pl.BlockSpec((2 * N_HEADS, tb), lambda bi, ti: (0, bi * nt + ti)),
        _resident((CONV_WIDTH, 3 * GROUP_WIDTH)),
        _resident((1, HEAD_DIM)), _resident((1, HEAD_DIM)),
        _resident((2 * N_HEADS, 1)), _resident((2 * N_HEADS, 1)),
        _resident((1, HEAD_DIM)),
        _resident((tb, tb)), _resident((tb, tb)), _resident((tb, tb)),
        _resident(((CONV_WIDTH - 1) * tb, tb)),
    ]
    hgrn_specs = [col_blk(4), col_blk(5), col_blk(6), col_blk(7),
                  _resident((depth, GROUP_WIDTH)), _resident((1, HEAD_DIM)),
                  _resident((tb, tb)), _resident((tb, tb))]
    assert len(gdn_specs) == N_GDN_IN and len(hgrn_specs) == N_HGRN_IN
    return pl.pallas_call(
        functools.partial(_mixer_kernel, layer),
        grid=(b, nt),
        in_specs=gdn_specs + hgrn_specs,
        out_specs=pl.BlockSpec((1, tb, 2 * GROUP_WIDTH), lambda bi, ti: (bi, ti, 0)),
        out_shape=jax.ShapeDtypeStruct((b, t, 2 * GROUP_WIDTH), BF16),
        scratch_shapes=[
            pltpu.VMEM((CONV_HALO, 3 * GROUP_WIDTH), F32),
            pltpu.VMEM((N_HEADS, HEAD_DIM, HEAD_DIM), F32),
            pltpu.VMEM((N_HEADS, HEAD_DIM, HEAD_DIM), F32),
        ],
        compiler_params=pltpu.CompilerParams(
            dimension_semantics=("arbitrary", "arbitrary"),
            vmem_limit_bytes=VMEM_LIMIT_BYTES),
        name="mixer",
    )(proj3, proj3, ab3, abt, convw, alog_row, dtb_row, alog_col, dtb_col, gdn_normw,
      ltri_c, utri_c, ones_c, shift,
      proj3, proj3, proj3, proj3, lb_logits, hgrn_normw, ltri, lvl)


def _post_mix_kernel(h_ref, mix_ref, p_ref, wout_ref, nw_ref, wup_ref, wdown_ref,
                     wgate_ref, wple_ref, out_ref):
    nw = nw_ref[...]
    subs = [(r, r + POST_SUB_TILE) for r in range(0, h_ref.shape[0], POST_SUB_TILE)]
    h1s, hns = [], []
    for r0, r1 in subs:
        mo = _dot(mix_ref[r0:r1, :], wout_ref[...])
        h1 = h_ref[r0:r1, :] + _rms(mo, nw[0:1, :])
        h1s.append(h1)
        hns.append(_rms(h1, nw[1:2, :]).astype(BF16))
    accs = [None] * len(subs)
    for j in range(wup_ref.shape[1] // FF_TILE):
        f0, f1 = j * FF_TILE, (j + 1) * FF_TILE
        for si in range(len(subs)):
            u = _dot(hns[si], wup_ref[:, f0:f1])
            a = jnp.square(jnp.maximum(u, 0.0)).astype(BF16)
            part = _dot(a, wdown_ref[f0:f1, :])
            accs[si] = part if accs[si] is None else accs[si] + part
    for si, (r0, r1) in enumerate(subs):
        h2 = h1s[si] + _rms(accs[si], nw[2:3, :])
        e = _rms(_dot(p_ref[r0:r1, :].astype(BF16), wple_ref[...]), nw[3:4, :])
        gate = _sigmoid(_dot(h2.astype(BF16), wgate_ref[...]))
        out_ref[r0:r1, :] = h2 + e * gate


def _post_mix(h2d, mix2d, p2d, w_out, norms, w_up, w_down, w_gate, w_ple):
    n, d = h2d.shape
    ff = w_up.shape[1]
    pd = p2d.shape[1]
    tile = lambda width: pl.BlockSpec((TOKEN_TILE, width), lambda i: (i, 0))
    return pl.pallas_call(
        _post_mix_kernel,
        grid=(n // TOKEN_TILE,),
        in_specs=[
            tile(d), tile(mix2d.shape[1]), tile(pd),
            _resident((d, d)), _resident((4, d)),
            _resident((d, ff)), _resident((ff, d)),
            _resident((d, d)), _resident((pd, d)),
        ],
        out_specs=tile(d),
        out_shape=jax.ShapeDtypeStruct((n, d), F32),
        compiler_params=pltpu.CompilerParams(
            dimension_semantics=("arbitrary",),
            vmem_limit_bytes=VMEM_LIMIT_BYTES),
        name="post_mix",
    )(h2d, mix2d, p2d, w_out, norms, w_up, w_down, w_gate, w_ple)


def kernel(x, p, pre_mix_norm, w_in, gdn_conv, gdn_a_log, gdn_dt_bias, gdn_norm, hgrn_lb_logits,
           hgrn_norm, w_out, post_mix_norm, pre_mlp_norm, w_mlp_up, w_mlp_down, post_mlp_norm,
           w_ple_proj, ple_norm, w_ple_gate):
    bsz, t_len, d = x.shape
    depth = w_in.shape[0]
    n = bsz * t_len
    main_cols = 4 * GROUP_WIDTH
    ab_off = main_cols
    hg_off = ab_off + 2 * N_HEADS
    assert w_in.shape[2] == hg_off + 4 * GROUP_WIDTH
    assert t_len % SEQ_BLOCK == 0 and n % TOKEN_TILE == 0

    pad_row = lambda v: jnp.pad(v.astype(F32), (0, HEAD_DIM - v.shape[0]))[None, :]
    pad_col = lambda v: jnp.pad(v.astype(F32), (0, 2 * N_HEADS - v.shape[0]))[:, None]

    h = x.reshape(n, d)
    for i in range(depth):
        wi = w_in[i]
        w_main = jnp.concatenate([wi[:, :main_cols], wi[:, hg_off:]], axis=1).astype(BF16)
        w_ab_cols = wi[:, ab_off:hg_off]
        w_ab = jnp.pad(w_ab_cols, ((0, 0), (0, HEAD_DIM - 2 * N_HEADS))).astype(BF16)
        w_abt = w_ab_cols.T.astype(BF16)

        proj, ab, abt = _in_proj(h, pre_mix_norm[i][None, :], w_main, w_ab, w_abt)
        proj3 = proj.reshape(bsz, t_len, proj.shape[1])
        ab3 = ab.reshape(bsz, t_len, HEAD_DIM)

        mix = _mixer(proj3, ab3, abt, gdn_conv[i].astype(F32),
                     pad_row(gdn_a_log[i]), pad_row(gdn_dt_bias[i]),
                     pad_col(gdn_a_log[i]), pad_col(gdn_dt_bias[i]),
                     gdn_norm[i][None, :].astype(F32),
                     hgrn_lb_logits.astype(F32), hgrn_norm[i][None, :].astype(F32), i)

        norms = jnp.stack([post_mix_norm[i], pre_mlp_norm[i], post_mlp_norm[i], ple_norm[i]]).astype(F32)
        h = _post_mix(h, mix.reshape(n, 2 * GROUP_WIDTH), p[i].reshape(n, p.shape[-1]),
                      w_out[i].astype(BF16), norms, w_mlp_up[i].astype(BF16),
                      w_mlp_down[i].astype(BF16), w_ple_gate[i].astype(BF16),
                      w_ple_proj[i].astype(BF16))
    return h.reshape(bsz, t_len, d)
```
